```python
import jax, jax.numpy as jnp
from jax import lax
import numpy as np

D_MODEL = 1024
BATCH = 32
SEQ = 256
DEPTH = 1
DEC_BATCH = 4
DEC_SEQ = 4096
PAST_LEN = 512

GRID_W = 64
CONV_WIDTH = 1024
HGRN_HEADS = 8
HGRN_DK = 128
HGRN_DV = 128
HGRN_KW = HGRN_HEADS * HGRN_DK
HGRN_VW = HGRN_HEADS * HGRN_DV
CHUNK = 32
N_EXPERTS = 16
EC_CAPACITY_FACTOR = 2
D_EXPERT = 1024
EPS = 1e-6
PROJ_WIDTHS = (CONV_WIDTH, CONV_WIDTH, CONV_WIDTH, HGRN_KW, HGRN_KW, HGRN_KW,
               HGRN_VW, HGRN_VW, D_MODEL, D_MODEL)
PROJ_TOTAL = sum(PROJ_WIDTHS)

kernel_name = "hybrid_diffusion_conv_hgrn2_ec_moe_step"


def rmsnorm(x, g):
    xf = x.astype(jnp.float32)
    y = xf * lax.rsqrt(jnp.mean(xf * xf, axis=-1, keepdims=True) + EPS)
    return (y * g.astype(jnp.float32)).astype(x.dtype)


def split_proj(p):
    offs, acc = [], 0
    for w in PROJ_WIDTHS[:-1]:
        acc += w
        offs.append(acc)
    return jnp.split(p, offs, axis=-1)


def short_conv(u, w, b, row_w):
    B, T, Ch = u.shape
    ur = u.reshape(B, T // row_w, row_w, Ch)
    up = jnp.pad(ur, ((0, 0), (0, 0), (1, 1), (0, 0)))
    y = up[:, :, :-2] * w[0] + up[:, :, 1:-1] * w[1] + up[:, :, 2:] * w[2] + b
    return y.reshape(B, T, Ch)


def hgrn2_scan(q, logf, v, s0):
    B, T, H, K = q.shape
    N = T // CHUNK
    k = -jnp.expm1(logf)
    causal = jnp.tril(jnp.ones((CHUNK, CHUNK), dtype=bool))[None, :, :, None, None]

    def to_chunks(a):
        return jnp.moveaxis(a.reshape(B, N, CHUNK, H, a.shape[-1]), 1, 0)

    def step(S, inp):
        qc, lc, kc, vc = inp
        bcum = jnp.cumsum(lc, axis=1)
        o_inter = jnp.einsum('bchk,bhkv->bchv', qc * jnp.exp(bcum), S)
        diff = bcum[:, :, None] - bcum[:, None, :]
        decay = jnp.exp(jnp.where(causal, diff, -jnp.inf))
        attn = jnp.einsum('btshk,bshk->bhts', qc[:, :, None] * decay, kc)
        o_intra = jnp.einsum('bhts,bshv->bthv', attn, vc)
        blast = bcum[:, -1]
        S_new = jnp.exp(blast)[..., None] * S + jnp.einsum(
            'bshk,bshv->bhkv', kc * jnp.exp(blast[:, None] - bcum), vc)
        return S_new, o_inter + o_intra

    S, o = lax.scan(step, s0, (to_chunks(q), to_chunks(logf), to_chunks(k), to_chunks(v)))
    o = jnp.moveaxis(o, 0, 1).reshape(B, T, H, v.shape[-1])
    return o, S


def hgrn2_branch(q_raw, zf, zb, i_raw, g_raw, lb, onorm, s0):
    B, T, _ = q_raw.shape
    f32 = jnp.float32
    q = jax.nn.silu(q_raw.astype(f32)).reshape(B, T, HGRN_HEADS, HGRN_DK)
    v = i_raw.astype(f32).reshape(B, T, HGRN_HEADS, HGRN_DV)
    logf_f = jnp.log(lb[0] + (1.0 - lb[0]) * jax.nn.sigmoid(zf.astype(f32)))
    logf_b = jnp.log(lb[1] + (1.0 - lb[1]) * jax.nn.sigmoid(zb.astype(f32)))
    logf_f = logf_f.reshape(B, T, HGRN_HEADS, HGRN_DK)
    logf_b = logf_b.reshape(B, T, HGRN_HEADS, HGRN_DK)
    s0 = s0.astype(f32)
    o_f, S_f = hgrn2_scan(q, logf_f, v, s0[:, 0])
    o_b, S_b = hgrn2_scan(q[:, ::-1], logf_b[:, ::-1], v[:, ::-1], s0[:, 1])
    o = o_f + o_b[:, ::-1]
    o = o * lax.rsqrt(jnp.mean(o * o, axis=-1, keepdims=True) + EPS)
    o = o * onorm.astype(f32).reshape(HGRN_HEADS, HGRN_DV)
    o = o * jax.nn.silu(g_raw.astype(f32)).reshape(B, T, HGRN_HEADS, HGRN_DV)
    return o.reshape(B, T, HGRN_VW), jnp.stack([S_f, S_b], axis=1)


def expert_choice_ffn(h, w_router, w_g, w_u, w_d):
    B, T, D = h.shape
    cap = EC_CAPACITY_FACTOR * T // N_EXPERTS
    aff = jax.nn.softmax(jnp.einsum('btd,de->bte', h.astype(jnp.float32),
                                    w_router.astype(jnp.float32)), axis=-1)
    top_w, top_idx = lax.top_k(jnp.swapaxes(aff, 1, 2), cap)
    bidx = jnp.arange(B)[:, None, None]
    xs = h[bidx, top_idx]
    hid = jax.nn.silu(jnp.einsum('becd,edf->becf', xs, w_g)) * jnp.einsum('becd,edf->becf', xs, w_u)
    ys = jnp.einsum('becf,efd->becd', hid, w_d) * top_w[..., None].astype(h.dtype)
    return jnp.zeros_like(h).at[bidx, top_idx].add(ys.astype(h.dtype))


def trunk_layer(x, cond, s0, row_w, lb, w_ada, b_ada, norm1, w_in, conv_w, conv_b,
                onorm, w_out_a, w_out_b, w_o, norm2, w_router, w_e_gate, w_e_up, w_e_down):
    D = x.shape[-1]
    mod = (jnp.dot(jax.nn.silu(cond), w_ada) + b_ada).reshape(-1, 1, 6 * D)
    sh1, sc1, g1, sh2, sc2, g2 = jnp.split(mod, 6, axis=-1)
    h = rmsnorm(x, norm1) * (1.0 + sc1) + sh1
    x_in, bg, cg, q, zf, zb, iv, og, ga, gb = split_proj(jnp.dot(h, w_in))
    ya = jnp.dot(bg * short_conv(cg * x_in, conv_w, conv_b, row_w), w_out_a)
    ob, S = hgrn2_branch(q, zf, zb, iv, og, lb, onorm, s0)
    yb = jnp.dot(ob.astype(x.dtype), w_out_b)
    mix = jnp.dot(jax.nn.sigmoid(ga) * ya + jax.nn.sigmoid(gb) * yb, w_o)
    x = x + g1 * mix
    h2 = rmsnorm(x, norm2) * (1.0 + sc2) + sh2
    x = x + g2 * expert_choice_ffn(h2, w_router, w_e_gate, w_e_up, w_e_down)
    return x, S


def setup_inputs(seed: int = 0) -> dict:
    key = jax.random.key(seed)
    ks = jax.random.split(key, 24)
    n = jax.random.normal
    D = D_MODEL
    return {
        'x_prompt': n(ks[0], (BATCH, SEQ, D), jnp.float32),
        'x_sample': n(ks[1], (DEC_BATCH, DEC_SEQ, D), jnp.float32),
        'state_hgrn': 0.5 * n(ks[2], (DEC_BATCH, DEPTH, 2, HGRN_HEADS, HGRN_DK, HGRN_DV), jnp.float32),
        'c': n(ks[3], (DEC_BATCH, D), jnp.float32),
        'c_ctx': n(ks[4], (D,), jnp.float32),
        'w_ada': 0.5 * D ** -0.5 * n(ks[5], (DEPTH, D, 6 * D), jnp.float32),
        'b_ada': 0.02 * n(ks[6], (DEPTH, 6 * D), jnp.float32),
        'norm1': 1.0 + 0.05 * n(ks[7], (DEPTH, D), jnp.float32),
        'w_in': D ** -0.5 * n(ks[8], (DEPTH, D, PROJ_TOTAL), jnp.float32),
        'conv_w': 0.5 * n(ks[9], (DEPTH, 3, CONV_WIDTH), jnp.float32),
        'conv_b': 0.02 * n(ks[10], (DEPTH, CONV_WIDTH), jnp.float32),
        'hgrn_lb': 0.1 * n(ks[11], (2, DEPTH + 1, HGRN_KW), jnp.float32),
        'hgrn_onorm': 1.0 + 0.05 * n(ks[12], (DEPTH, HGRN_VW), jnp.float32),
        'w_out_a': CONV_WIDTH ** -0.5 * n(ks[13], (DEPTH, CONV_WIDTH, D), jnp.float32),
        'w_out_b': HGRN_VW ** -0.5 * n(ks[14], (DEPTH, HGRN_VW, D), jnp.float32),
        'w_o': D ** -0.5 * n(ks[15], (DEPTH, D, D), jnp.float32),
        'norm2': 1.0 + 0.05 * n(ks[16], (DEPTH, D), jnp.float32),
        'w_router': D ** -0.5 * n(ks[17], (DEPTH, D, N_EXPERTS), jnp.float32),
        'w_e_gate': D ** -0.5 * n(ks[18], (DEPTH, N_EXPERTS, D, D_EXPERT), jnp.float32),
        'w_e_up': D ** -0.5 * n(ks[19], (DEPTH, N_EXPERTS, D, D_EXPERT), jnp.float32),
        'w_e_down': D_EXPERT ** -0.5 * n(ks[20], (DEPTH, N_EXPERTS, D_EXPERT, D), jnp.float32),
        'final_norm': 1.0 + 0.05 * n(ks[21], (D,), jnp.float32),
    }


def reference(x_prompt, x_sample, state_hgrn, c, c_ctx, w_ada, b_ada, norm1, w_in, conv_w,
              conv_b, hgrn_lb, hgrn_onorm, w_out_a, w_out_b, w_o, norm2, w_router,
              w_e_gate, w_e_up, w_e_down, final_norm):
    lb_all = jnp.cumsum(jax.nn.softmax(hgrn_lb.astype(jnp.float32), axis=1), axis=1)
    B_p, T_p, _ = x_prompt.shape
    xp = x_prompt
    xs = x_sample
    s_zero = jnp.zeros((B_p, 2, HGRN_HEADS, HGRN_DK, HGRN_DV), jnp.float32)
    ctx_states = []
    for l in range(DEPTH):
        shared = (lb_all[:, l], w_ada[l], b_ada[l], norm1[l], w_in[l], conv_w[l], conv_b[l],
                  hgrn_onorm[l], w_out_a[l], w_out_b[l], w_o[l], norm2[l], w_router[l],
                  w_e_gate[l], w_e_up[l], w_e_down[l])
        xp, S_ctx = trunk_layer(xp, c_ctx, s_zero, T_p, *shared)
        ctx_states.append(S_ctx.astype(x_prompt.dtype))
        xs, _ = trunk_layer(xs, c, state_hgrn[:, l], GRID_W, *shared)
    y_prompt = rmsnorm(xp, final_norm)
    y_sample = rmsnorm(xs, final_norm)
    new_state_hgrn = jnp.stack(ctx_states, axis=1)
    return (y_prompt, y_sample, new_state_hgrn)
```

```python
import functools

import jax
import jax.numpy as jnp
from jax import lax
from jax.experimental import pallas as pl
from jax.experimental.pallas import tpu as pltpu

F32 = jnp.float32
BF16 = jnp.bfloat16
EPS = 1e-6
N_HEADS = 8
HEAD_DIM = 128
N_EXPERTS = 16
CAPACITY_FACTOR = 2
LANES = 128
MIB = 1024 * 1024

PROJ_TILE = 512
SCAN_STEP = 256
SCAN_CHUNK = 64
FFN_ROWS = 512
SUBLANES = 8
PREFIX_SEG = 256

NT_DIMS = (((1,), (1,)), ((), ()))
TN_DIMS = (((0,), (0,)), ((), ()))


def _params(semantics, vmem_mib):
    return pltpu.CompilerParams(dimension_semantics=semantics, vmem_limit_bytes=vmem_mib * MIB)


def _dot(a, b):
    return jnp.dot(a, b, preferred_element_type=F32)


def _split2(x):
    hi = x.astype(BF16)
    lo = (x - hi.astype(F32)).astype(BF16)
    return hi, lo


def _split3(x):
    hi = x.astype(BF16)
    r1 = x - hi.astype(F32)
    mid = r1.astype(BF16)
    lo = (r1 - mid.astype(F32)).astype(BF16)
    return hi, mid, lo


def _rms(x):
    return x * lax.rsqrt(jnp.mean(x * x, axis=-1, keepdims=True) + EPS)


def _adaln_body(cond_ref, w_ref, b_ref, o_ref):
    c = cond_ref[...]
    s_hi, s_lo = _split2(c * jax.nn.sigmoid(c))
    w_hi, w_lo = _split2(w_ref[...])
    o_ref[...] = _dot(s_hi, w_hi) + _dot(s_lo, w_hi) + _dot(s_hi, w_lo) + b_ref[...]


def _adaln(cond, w, b):
    rows, d = cond.shape
    n = w.shape[1]
    return pl.pallas_call(
        _adaln_body,
        grid=(n // d,),
        in_specs=[pl.BlockSpec((rows, d), lambda j: (0, 0)),
                  pl.BlockSpec((d, d), lambda j: (0, j)),
                  pl.BlockSpec((1, d), lambda j: (0, j))],
        out_specs=pl.BlockSpec((rows, d), lambda j: (0, j)),
        out_shape=jax.ShapeDtypeStruct((rows, n), F32),
        compiler_params=_params(("arbitrary",), 32),
        name="adaln",
    )(cond, w, b)


def _mod_row(i, tiles_per_batch, ctx_row):
    return ctx_row if tiles_per_batch is None else i // tiles_per_batch


def _store_heads(ref, val):
    for h in range(N_HEADS):
        ref[h] = val[:, h * HEAD_DIM:(h + 1) * HEAD_DIM].astype(ref.dtype)


def _proj_body(x_ref, mod_ref, n1_ref, hlb_ref, cw_ref, cb_ref, w_ref, woa_ref,
               za_ref, q_ref, lff_ref, lfb_ref, v_ref, g_ref, sgb_ref,
               h_scr, a_scr, *, tiles_per_batch, ctx_row, row_w):
    i = pl.program_id(0)
    j = pl.program_id(1)
    tm, d = x_ref.shape

    @pl.when(j == 0)
    def _():
        row = _mod_row(i, tiles_per_batch, ctx_row)
        sh = mod_ref[pl.ds(row, 1), 0:d]
        sc = mod_ref[pl.ds(row, 1), d:2 * d]
        h_scr[...] = (_rms(x_ref[...]) * n1_ref[...] * (1.0 + sc) + sh).astype(BF16)

    p = _dot(h_scr[...], w_ref[...])

    def log_forget(first):
        lb = jax.nn.sigmoid(hlb_ref[first:first + 1, :] - hlb_ref[first + 1:first + 2, :])
        return jnp.log(lb + (1.0 - lb) * jax.nn.sigmoid(p))

    @pl.when(j == 0)
    def _():
        a_scr[...] = p

    @pl.when(j == 1)
    def _():
        u = p * a_scr[...]
        t = lax.broadcasted_iota(jnp.int32, (tm, 1), 0) & (row_w - 1)
        left = jnp.where(t == 0, 0.0, pltpu.roll(u, 1, 0))
        right = jnp.where(t == row_w - 1, 0.0, pltpu.roll(u, tm - 1, 0))
        a_scr[...] = (left * cw_ref[0:1, :] + u * cw_ref[1:2, :] + right * cw_ref[2:3, :]
                      + cb_ref[...])

    @pl.when(j == 2)
    def _():
        a_scr[...] = _dot((p * a_scr[...]).astype(BF16), woa_ref[...])

    @pl.when(j == 3)
    def _():
        za_ref[...] = (jax.nn.sigmoid(p) * a_scr[...]).astype(za_ref.dtype)

    @pl.when(j == 4)
    def _():
        _store_heads(q_ref, p * jax.nn.sigmoid(p))

    @pl.when(j == 5)
    def _():
        _store_heads(lff_ref, log_forget(0))

    @pl.when(j == 6)
    def _():
        _store_heads(lfb_ref, log_forget(2))

    @pl.when(j == 7)
    def _():
        _store_heads(v_ref, p)

    @pl.when(j == 8)
    def _():
        _store_heads(g_ref, p * jax.nn.sigmoid(p))

    @pl.when(j == 9)
    def _():
        sgb_ref[...] = jax.nn.sigmoid(p).astype(sgb_ref.dtype)


PROJ_ORDER = (0, 2, 1, 8, 3, 4, 5, 6, 7, 9)


def _proj(x, mod, norm1, hlb, conv_w, conv_b, w_in_p, w_out_a, *, tokens_per_batch, ctx_row, row_w):
    ntok, d = x.shape
    tm = PROJ_TILE
    ngroups = w_in_p.shape[1] // d
    tiles_per_batch = None if tokens_per_batch is None else tokens_per_batch // tm
    tok = pl.BlockSpec((tm, d), lambda i, j: (i, 0))
    heads = pl.BlockSpec((N_HEADS, tm, HEAD_DIM), lambda i, j: (0, i, 0))
    full = lambda a: pl.BlockSpec(a.shape, lambda i, j: (0,) * a.ndim)
    head_shape = lambda dt: jax.ShapeDtypeStruct((N_HEADS, ntok, HEAD_DIM), dt)
    return pl.pallas_call(
        functools.partial(_proj_body, tiles_per_batch=tiles_per_batch, ctx_row=ctx_row, row_w=row_w),
        grid=(ntok // tm, ngroups),
        in_specs=[tok, full(mod), full(norm1), full(hlb), full(conv_w), full(conv_b),
                  pl.BlockSpec((d, d), lambda i, j: (0, j)), full(w_out_a)],
        out_specs=[tok, heads, heads, heads, heads, heads, tok],
        out_shape=[jax.ShapeDtypeStruct((ntok, d), BF16), head_shape(BF16), head_shape(F32),
                   head_shape(F32), head_shape(BF16), head_shape(BF16),
                   jax.ShapeDtypeStruct((ntok, d), BF16)],
        scratch_shapes=[pltpu.VMEM((tm, d), BF16), pltpu.VMEM((tm, d), F32)],
        compiler_params=_params(("arbitrary", "arbitrary"), 48),
        name="proj",
    )(x, mod, norm1, hlb, conv_w, conv_b, w_in_p, w_out_a)


def _scan_body(*refs, has_s0, has_sout):
    qf_ref, qb_ref, lff_ref, lfb_ref, vf_ref, vb_ref = refs[:6]
    k = 6
    s0_ref = so_ref = None
    if has_s0:
        s0_ref = refs[k]
        k += 1
    of_ref, ob_ref = refs[k], refs[k + 1]
    k += 2
    if has_sout:
        so_ref = refs[k]
        k += 1
    st_scr = refs[k]
    step = pl.program_id(1)
    c_len = SCAN_CHUNK
    n_chunks = qf_ref.shape[2] // c_len

    @pl.when(step == 0)
    def _():
        if has_s0:
            for dd in range(2):
                for h in range(N_HEADS):
                    st_scr[dd, h] = s0_ref[0, dd, h].T
        else:
            st_scr[...] = jnp.zeros_like(st_scr)

    row = lax.broadcasted_iota(jnp.int32, (c_len, c_len), 0)
    col = lax.broadcasted_iota(jnp.int32, (c_len, c_len), 1)
    keep_f = row >= col
    keep_b = row <= col

    def chunk(q_ref, lf_ref, v_ref, o_ref, h, c, dd, keep, last):
        sl = pl.ds(c * c_len, c_len)
        q = q_ref[h, 0, sl, :].astype(F32)
        lf = lf_ref[h, 0, sl, :]
        v = v_ref[h, 0, sl, :]
        tri = keep.astype(BF16)
        hi, mid, lo = _split3(lf)
        b = _dot(tri, hi) + _dot(tri, mid) + _dot(tri, lo)
        blast = b[last:last + 1, :]
        ref = 0.5 * blast
        kk = 1.0 - jnp.exp(lf)
        st = st_scr[dd, h]
        qe = (q * jnp.exp(b)).astype(BF16)
        o = lax.dot_general(qe, st.astype(BF16), NT_DIMS, preferred_element_type=F32)
        qs = (q * jnp.exp(b - ref)).astype(BF16)
        ks = (kk * jnp.exp(ref - b)).astype(BF16)
        attn = lax.dot_general(qs, ks, NT_DIMS, preferred_element_type=F32)
        attn = jnp.where(keep, attn, 0.0).astype(BF16)
        o = o + _dot(attn, v)
        o_ref[h, 0, sl, :] = o.astype(o_ref.dtype)
        ke = (kk * jnp.exp(blast - b)).astype(BF16)
        st_scr[dd, h] = st * jnp.exp(blast) + lax.dot_general(v, ke, TN_DIMS,
                                                               preferred_element_type=F32)

    def head(h, carry):
        for c in range(n_chunks):
            chunk(qf_ref, lff_ref, vf_ref, of_ref, h, c, 0, keep_f, c_len - 1)
            chunk(qb_ref, lfb_ref, vb_ref, ob_ref, h, n_chunks - 1 - c, 1, keep_b, 0)
        return carry

    lax.fori_loop(0, N_HEADS, head, 0)

    if has_sout:
        @pl.when(step == pl.num_programs(1) - 1)
        def _():
            for dd in range(2):
                for h in range(N_HEADS):
                    so_ref[0, dd, h] = st_scr[dd, h].T


def _scan(q, lff, lfb, v, s0, *, batch, want_state):
    nh, ntok, hd = q.shape
    t = ntok // batch
    ts = SCAN_STEP
    ns = t // ts
    shape4 = (nh, batch, t, hd)
    q, lff, lfb, v = (a.reshape(shape4) for a in (q, lff, lfb, v))
    fwd = pl.BlockSpec((nh, 1, ts, hd), lambda b, s: (0, b, s, 0))
    bwd = pl.BlockSpec((nh, 1, ts, hd), lambda b, s: (0, b, ns - 1 - s, 0))
    state = pl.BlockSpec((1, 2, nh, hd, hd), lambda b, s: (b, 0, 0, 0, 0))
    in_specs = [fwd, bwd, fwd, bwd, fwd, bwd]
    args = [q, q, lff, lfb, v, v]
    if s0 is not None:
        in_specs.append(state)
        args.append(s0)
    out_specs = [fwd, bwd]
    out_shape = [jax.ShapeDtypeStruct(shape4, BF16), jax.ShapeDtypeStruct(shape4, BF16)]
    if want_state:
        out_specs.append(state)
        out_shape.append(jax.ShapeDtypeStruct((batch, 2, nh, hd, hd), F32))
    outs = pl.pallas_call(
        functools.partial(_scan_body, has_s0=s0 is not None, has_sout=want_state),
        grid=(batch, ns),
        in_specs=in_specs,
        out_specs=out_specs,
        out_shape=out_shape,
        scratch_shapes=[pltpu.VMEM((2, nh, hd, hd), F32)],
        compiler_params=_params(("arbitrary", "arbitrary"), 32),
        name="scan",
    )(*args)
    o_f = outs[0].reshape(nh, ntok, hd)
    o_b = outs[1].reshape(nh, ntok, hd)
    return o_f, o_b, (outs[2] if want_state else None)


def _post_body(of_ref, ob_ref, g_ref, sgb_ref, za_ref, x_ref, mod_ref, on_ref, n2_ref,
               wob_ref, wo_ref, wrt_ref, x1_ref, h2p_ref, lgt_ref, ob_scr,
               *, tiles_per_batch, ctx_row):
    i = pl.program_id(0)
    tm, d = x_ref.shape
    row = _mod_row(i, tiles_per_batch, ctx_row)
    for h in range(N_HEADS):
        lanes = slice(h * HEAD_DIM, (h + 1) * HEAD_DIM)
        o = _rms(of_ref[h].astype(F32) + ob_ref[h].astype(F32))
        ob_scr[:, lanes] = (o * on_ref[:, lanes] * g_ref[h].astype(F32)).astype(BF16)
    yb = _dot(ob_scr[...], wob_ref[...])
    z = za_ref[...].astype(F32) + sgb_ref[...].astype(F32) * yb
    mix = _dot(z.astype(BF16), wo_ref[...])
    x1 = x_ref[...] + mod_ref[pl.ds(row, 1), 2 * d:3 * d] * mix
    x1_ref[...] = x1
    sh2 = mod_ref[pl.ds(row, 1), 3 * d:4 * d]
    sc2 = mod_ref[pl.ds(row, 1), 4 * d:5 * d]
    h2 = _rms(x1) * n2_ref[...] * (1.0 + sc2) + sh2
    half = d // 2
    top = lax.bitcast_convert_type(h2[:, :half].astype(BF16).astype(F32), jnp.uint32)
    bot = lax.bitcast_convert_type(h2[:, half:].astype(BF16).astype(F32), jnp.uint32)
    h2p_ref[...] = top | (bot >> 16)
    h_hi, h_lo = _split2(h2)
    w_hi, w_lo = _split2(wrt_ref[...])
    nt = lambda a, b: lax.dot_general(a, b, NT_DIMS, preferred_element_type=F32)
    lgt_ref[...] = nt(w_hi, h_hi) + nt(w_lo, h_hi) + nt(w_hi, h_lo)


def _post(o_f, o_b, g, sgb, za, x, mod, onorm, norm2, w_out_b, w_o, w_router_t,
          *, tokens_per_batch, ctx_row):
    ntok, d = x.shape
    tm = PROJ_TILE
    ne = w_router_t.shape[0]
    tiles_per_batch = None if tokens_per_batch is None else tokens_per_batch // tm
    tok = pl.BlockSpec((tm, d), lambda i: (i, 0))
    heads = pl.BlockSpec((N_HEADS, tm, HEAD_DIM), lambda i: (0, i, 0))
    full = lambda a: pl.BlockSpec(a.shape, lambda i: (0,) * a.ndim)
    return pl.pallas_call(
        functools.partial(_post_body, tiles_per_batch=tiles_per_batch, ctx_row=ctx_row),
        grid=(ntok // tm,),
        in_specs=[heads, heads, heads, tok, tok, tok, full(mod), full(onorm), full(norm2),
                  full(w_out_b), full(w_o), full(w_router_t)],
        out_specs=[tok, pl.BlockSpec((tm, d // 2), lambda i: (i, 0)),
                   pl.BlockSpec((ne, tm), lambda i: (0, i))],
        out_shape=[jax.ShapeDtypeStruct((ntok, d), F32),
                   jax.ShapeDtypeStruct((ntok, d // 2), jnp.uint32),
                   jax.ShapeDtypeStruct((ne, ntok), F32)],
        scratch_shapes=[pltpu.VMEM((tm, d), BF16)],
        compiler_params=_params(("arbitrary",), 48),
        name="post",
    )(o_f, o_b, g, sgb, za, x, mod, onorm, norm2, w_out_b, w_o, w_router_t)


def _lane_prefix(mask):
    rows, t = mask.shape
    seg = min(PREFIX_SEG, t)
    r = lax.broadcasted_iota(jnp.int32, (seg, seg), 0)
    c = lax.broadcasted_iota(jnp.int32, (seg, seg), 1)
    tri = (r <= c).astype(BF16)
    carry = jnp.zeros((rows, 1), F32)
    outs = []
    for s in range(t // seg):
        loc = _dot(mask[:, s * seg:(s + 1) * seg].astype(BF16), tri) + carry
        outs.append(loc)
        carry = loc[:, seg - 1:seg]
    return outs[0] if len(outs) == 1 else jnp.concatenate(outs, axis=1)


def _route_body(lg_ref, idx_ref, w_ref, pos_scr, aff_scr, *, cap):
    ne, t = lg_ref.shape
    lg = lg_ref[...]
    e = jnp.exp(lg - jnp.max(lg, axis=0, keepdims=True))
    aff = e / jnp.sum(e, axis=0, keepdims=True)
    bits = lax.bitcast_convert_type(aff, jnp.int32)

    def search(it, tau):
        cand = tau | (jnp.int32(1) << (30 - it))
        cnt = jnp.sum((bits >= cand).astype(F32), axis=1, keepdims=True)
        return jnp.where(cnt >= cap, cand, tau)

    tau = lax.fori_loop(0, 31, search, jnp.zeros((ne, 1), jnp.int32))
    above = bits > tau
    ties = bits == tau
    need = cap - jnp.sum(above.astype(F32), axis=1, keepdims=True)
    sel = above | (ties & (_lane_prefix(ties) <= need))
    pos = jnp.where(sel, _lane_prefix(sel) - 1.0, -1.0)
    for c in range(t // LANES):
        pos_scr[c] = pos[:, c * LANES:(c + 1) * LANES]
        aff_scr[c] = aff[:, c * LANES:(c + 1) * LANES]

    lane = lax.broadcasted_iota(jnp.int32, (1, LANES), 1).astype(F32)
    slot0 = lax.broadcasted_iota(jnp.int32, (LANES, LANES), 0).astype(F32)
    width = min(cap, LANES)

    def expert(ex, carry):
        for sb in range(max(cap // LANES, 1)):
            slot = slot0 + float(sb * LANES)

            def lanes_chunk(c, acc):
                hit = pos_scr[c, pl.ds(ex, 1), :] == slot
                tok = lane + (c * LANES).astype(F32)
                return (acc[0] + jnp.where(hit, tok, 0.0),
                        acc[1] + jnp.where(hit, aff_scr[c, pl.ds(ex, 1), :], 0.0))

            zero = jnp.zeros((LANES, LANES), F32)
            acc_i, acc_w = lax.fori_loop(0, t // LANES, lanes_chunk, (zero, zero))
            row_i = jnp.sum(acc_i.T, axis=0, keepdims=True)
            row_w = jnp.sum(acc_w.T, axis=0, keepdims=True)
            out = slice(sb * LANES, sb * LANES + width)
            idx_ref[0, ex, :, out] = row_i[:, :width].astype(jnp.int32)
            w_ref[0, ex, :, out] = row_w[:, :width]
        return carry

    lax.fori_loop(0, ne, expert, 0)


def _route(logits_t, *, batch):
    ne, ntok = logits_t.shape
    t = ntok // batch
    cap = CAPACITY_FACTOR * t // ne
    out = pl.BlockSpec((1, ne, 1, cap), lambda b: (b, 0, 0, 0))
    return pl.pallas_call(
        functools.partial(_route_body, cap=cap),
        grid=(batch,),
        in_specs=[pl.BlockSpec((ne, t), lambda b: (0, b))],
        out_specs=[out, out],
        out_shape=[jax.ShapeDtypeStruct((batch, ne, 1, cap), jnp.int32),
                   jax.ShapeDtypeStruct((batch, ne, 1, cap), F32)],
        scratch_shapes=[pltpu.VMEM((t // LANES, ne, LANES), F32),
                        pltpu.VMEM((t // LANES, ne, LANES), F32)],
        compiler_params=_params(("arbitrary",), 32),
        name="route",
    )(logits_t)


def _gather_body(idx_sm, h2p_ref, *rest, experts_per_step, cap, ne):
    xs_ref = rest[-1]
    half = xs_ref.shape[2]
    b = pl.program_id(0)
    g = pl.program_id(1)
    sub = lax.broadcasted_iota(jnp.int32, (SUBLANES, half), 0)
    for k in range(experts_per_step):
        base = (b * ne + g * experts_per_step + k) * cap

        def rows(jj, carry, base=base, k=k):
            j0 = pl.multiple_of(jj * SUBLANES, SUBLANES)
            tile = None
            for u in range(SUBLANES):
                src = jnp.broadcast_to(h2p_ref[0, pl.ds(idx_sm[base + j0 + u], 1), :],
                                       (SUBLANES, half))
                tile = src if tile is None else jnp.where(sub == u, src, tile)
            xs_ref[k, pl.ds(j0, SUBLANES), :] = tile
            return carry

        lax.fori_loop(0, cap // SUBLANES, rows, 0)


def _gather(idx, h2p, xs_prev, *, batch, experts_per_step, total_rows, row_offset):
    _, ne, _, cap = idx.shape
    ntok, half = h2p.shape
    t = ntok // batch
    eps = experts_per_step
    off = row_offset // cap
    in_specs = [pl.BlockSpec((1, t, half), lambda b, g, sm: (b, 0, 0))]
    args = [idx.reshape(-1), h2p.reshape(batch, t, half)]
    aliases = {}
    if xs_prev is not None:
        in_specs.append(pl.BlockSpec(memory_space=pl.ANY))
        args.append(xs_prev)
        aliases = {2: 0}
    return pl.pallas_call(
        functools.partial(_gather_body, experts_per_step=eps, cap=cap, ne=ne),
        grid_spec=pltpu.PrefetchScalarGridSpec(
            num_scalar_prefetch=1,
            grid=(batch, ne // eps),
            in_specs=in_specs,
            out_specs=pl.BlockSpec((eps, cap, half), lambda b, g, sm: (g, off + b, 0)),
        ),
        out_shape=jax.ShapeDtypeStruct((ne, total_rows, half), jnp.uint32),
        input_output_aliases=aliases,
        compiler_params=_params(("arbitrary", "arbitrary"), 40),
        name="gather",
    )(*args)


def _ffn_body(xs_ref, wg_ref, wu_ref, wd_ref, ys_ref, wg_scr, wu_scr, wd_scr):
    @pl.when(pl.program_id(1) == 0)
    def _():
        wg_scr[...] = wg_ref[0, 0].astype(BF16)
        wu_scr[...] = wu_ref[0, 0].astype(BF16)
        wd_scr[...] = wd_ref[0, 0].astype(BF16)

    w = xs_ref[0]
    top = lax.bitcast_convert_type(w & jnp.uint32(0xFFFF0000), F32).astype(BF16)
    bot = lax.bitcast_convert_type(w << 16, F32).astype(BF16)
    xs = jnp.concatenate([top, bot], axis=1)
    gate = _dot(xs, wg_scr[...])
    up = _dot(xs, wu_scr[...])
    hid = (gate * jax.nn.sigmoid(gate) * up).astype(BF16)
    ys_ref[0] = _dot(hid, wd_scr[...]).astype(ys_ref.dtype)


def _ffn(xs, w_gate, w_up, w_down):
    ne, rows, half = xs.shape
    d, f = w_gate.shape[2], w_gate.shape[3]
    tr = FFN_ROWS
    wspec = lambda a: pl.BlockSpec((1, 1) + a.shape[2:], lambda e, r: (0, e, 0, 0))
    return pl.pallas_call(
        _ffn_body,
        grid=(ne, rows // tr),
        in_specs=[pl.BlockSpec((1, tr, half), lambda e, r: (e, r, 0)),
                  wspec(w_gate), wspec(w_up), wspec(w_down)],
        out_specs=pl.BlockSpec((1, tr, d), lambda e, r: (e, r, 0)),
        out_shape=jax.ShapeDtypeStruct((ne, rows, d), BF16),
        scratch_shapes=[pltpu.VMEM((d, f), BF16), pltpu.VMEM((d, f), BF16), pltpu.VMEM((f, d), BF16)],
        compiler_params=_params(("arbitrary", "arbitrary"), 56),
        name="ffn",
    )(xs, w_gate, w_up, w_down)


def _scatter_body(idx_sm, w_sm, ys_ref, y_ref, row_scr, *, experts_per_step, cap, ne):
    b = pl.program_id(0)
    g = pl.program_id(1)
    d = y_ref.shape[2]
    group = min(SUBLANES, cap // SUBLANES)
    stride = cap // group
    sub = lax.broadcasted_iota(jnp.int32, (SUBLANES, d), 0)

    @pl.when(g == 0)
    def _():
        y_ref[...] = jnp.zeros_like(y_ref)

    for k in range(experts_per_step):
        row_scr[...] = ys_ref[k].astype(F32)
        base = (b * ne + g * experts_per_step + k) * cap

        def rows(j, carry, base=base):
            pending = []
            for u in range(group):
                slot = j + u * stride
                tok = idx_sm[base + slot]
                t0 = pl.multiple_of((tok // SUBLANES) * SUBLANES, SUBLANES)
                src = jnp.broadcast_to(row_scr[pl.ds(slot, 1), :], (SUBLANES, d)) * w_sm[base + slot]
                tile = y_ref[0, pl.ds(t0, SUBLANES), :]
                pending.append((t0, tile + jnp.where(sub == tok % SUBLANES, src, 0.0)))
            for t0, val in pending:
                y_ref[0, pl.ds(t0, SUBLANES), :] = val
            return carry

        lax.fori_loop(0, stride, rows, 0)


def _scatter(idx, w, ys, *, batch, tokens_per_batch, experts_per_step, row_offset):
    _, ne, _, cap = idx.shape
    d = ys.shape[2]
    t = tokens_per_batch
    eps = experts_per_step
    off = row_offset // cap
    return pl.pallas_call(
        functools.partial(_scatter_body, experts_per_step=eps, cap=cap, ne=ne),
        grid_spec=pltpu.PrefetchScalarGridSpec(
            num_scalar_prefetch=2,
            grid=(batch, ne // eps),
            in_specs=[pl.BlockSpec((eps, cap, d), lambda b, g, i_sm, w_sm: (g, off + b, 0))],
            out_specs=pl.BlockSpec((1, t, d), lambda b, g, i_sm, w_sm: (b, 0, 0)),
            scratch_shapes=[pltpu.VMEM((cap, d), F32)],
        ),
        out_shape=jax.ShapeDtypeStruct((batch, t, d), F32),
        compiler_params=_params(("arbitrary", "arbitrary"), 48),
        name="scatter",
    )(idx.reshape(-1), w.reshape(-1), ys)


def _final_body(x1_ref, y_ref, mod_ref, fn_ref, o_ref, *, tiles_per_batch, ctx_row):
    d = x1_ref.shape[1]
    row = _mod_row(pl.program_id(0), tiles_per_batch, ctx_row)
    x = x1_ref[...] + mod_ref[pl.ds(row, 1), 5 * d:6 * d] * y_ref[...]
    o_ref[...] = _rms(x) * fn_ref[...]


def _final(x1, y, mod, final_norm, *, tokens_per_batch, ctx_row):
    ntok, d = x1.shape
    tm = PROJ_TILE
    tiles_per_batch = None if tokens_per_batch is None else tokens_per_batch // tm
    tok = pl.BlockSpec((tm, d), lambda i: (i, 0))
    full = lambda a: pl.BlockSpec(a.shape, lambda i: (0,) * a.ndim)
    return pl.pallas_call(
        functools.partial(_final_body, tiles_per_batch=tiles_per_batch, ctx_row=ctx_row),
        grid=(ntok // tm,),
        in_specs=[tok, tok, full(mod), full(final_norm)],
        out_specs=tok,
        out_shape=jax.ShapeDtypeStruct((ntok, d), F32),
        compiler_params=_params(("arbitrary",), 32),
        name="final",
    )(x1, y, mod, final_norm)


def kernel(x_prompt, x_sample, state_hgrn, c, c_ctx, w_ada, b_ada, norm1, w_in, conv_w, conv_b,
           hgrn_lb, hgrn_onorm, w_out_a, w_out_b, w_o, norm2, w_router, w_e_gate, w_e_up, w_e_down,
           final_norm):
    assert w_ada.shape[0] == 1, "single-layer trunk"
    bp, tp, d = x_prompt.shape
    bs, ts, _ = x_sample.shape
    grid_w = 64
    ne = w_router.shape[2]
    cap_p = CAPACITY_FACTOR * tp // ne
    cap_s = CAPACITY_FACTOR * ts // ne

    ctx_row = bs
    pad = (-(bs + 1)) % 8
    cond = jnp.concatenate([c, c_ctx[None, :], jnp.zeros((pad, d), F32)], axis=0)
    mod = _adaln(cond, w_ada[0], b_ada)

    w_in_p = jnp.concatenate([w_in[0][:, g * d:(g + 1) * d] for g in PROJ_ORDER], axis=1).astype(BF16)
    woa, wob, wo = (a[0].astype(BF16) for a in (w_out_a, w_out_b, w_o))
    hlb = hgrn_lb.reshape(-1, d)
    w_router_t = w_router[0].T

    def mix_half(x, s0, tokens_per_batch, row_w, want_state):
        batch = x.shape[0]
        xf = x.reshape(-1, d)
        za, q, lff, lfb, v, g, sgb = _proj(xf, mod, norm1, hlb, conv_w[0], conv_b, w_in_p, woa,
                                           tokens_per_batch=tokens_per_batch, ctx_row=ctx_row,
                                           row_w=row_w)
        o_f, o_b, s_new = _scan(q, lff, lfb, v, s0, batch=batch, want_state=want_state)
        x1, h2p, lgt = _post(o_f, o_b, g, sgb, za, xf, mod, hgrn_onorm, norm2, wob, wo, w_router_t,
                             tokens_per_batch=tokens_per_batch, ctx_row=ctx_row)
        idx, w = _route(lgt, batch=batch)
        return x1, h2p, idx, w, s_new

    x1p, h2pp, idxp, wp, s_ctx = mix_half(x_prompt, None, None, tp, True)
    x1s, h2ps, idxs, ws, _ = mix_half(x_sample, state_hgrn[:, 0], ts, grid_w, False)

    rows_p = bp * cap_p
    total_rows = rows_p + bs * cap_s
    xs = _gather(idxp, h2pp, None, batch=bp, experts_per_step=ne, total_rows=total_rows, row_offset=0)
    xs = _gather(idxs, h2ps, xs, batch=bs, experts_per_step=1, total_rows=total_rows,
                 row_offset=rows_p)
    ys = _ffn(xs, w_e_gate, w_e_up, w_e_down)
    yp = _scatter(idxp, wp, ys, batch=bp, tokens_per_batch=tp, experts_per_step=ne, row_offset=0)
    ysm = _scatter(idxs, ws, ys, batch=bs, tokens_per_batch=ts, experts_per_step=1,
                   row_offset=rows_p)

    y_prompt = _final(x1p, yp.reshape(-1, d), mod, final_norm[None, :], tokens_per_batch=None,
                      ctx_row=ctx_row)
    y_sample = _final(x1s, ysm.reshape(-1, d), mod, final_norm[None, :], tokens_per_batch=ts,
                      ctx_row=ctx_row)
    return (y_prompt.reshape(bp, tp, d), y_sample.reshape(bs, ts, d), s_ctx[:, None])
```

```python
import functools

import jax
import jax.numpy as jnp
from jax import lax
from jax.experimental import pallas as pl
from jax.experimental.pallas import tpu as pltpu

F32 = jnp.float32
BF16 = jnp.bfloat16
EPS = 1e-6
N_HEADS = 8
HEAD_DIM = 128
N_EXPERTS = 16
CAPACITY_FACTOR = 2
LANES = 128
MIB = 1024 * 1024

PROJ_ROWS = 256
PROJ_TILE = 512
SCAN_STEP = 512
SCAN_CHUNK = 64
FFN_ROWS = 512
SUBLANES = 8
PREFIX_SEG = 256

NT_DIMS = (((1,), (1,)), ((), ()))
TN_DIMS = (((0,), (0,)), ((), ()))


def _params(semantics, vmem_mib):
    return pltpu.CompilerParams(dimension_semantics=semantics, vmem_limit_bytes=vmem_mib * MIB)


def _dot(a, b):
    return jnp.dot(a, b, preferred_element_type=F32)


def _split2(x):
    hi = x.astype(BF16)
    lo = (x - hi.astype(F32)).astype(BF16)
    return hi, lo


def _split3(x):
    hi = x.astype(BF16)
    r1 = x - hi.astype(F32)
    mid = r1.astype(BF16)
    lo = (r1 - mid.astype(F32)).astype(BF16)
    return hi, mid, lo


def _rms(x):
    return x * lax.rsqrt(jnp.mean(x * x, axis=-1, keepdims=True) + EPS)


def _adaln_body(cond_ref, w_ref, b_ref, o_ref):
    c = cond_ref[...]
    s_hi, s_lo = _split2(c * jax.nn.sigmoid(c))
    w_hi, w_lo = _split2(w_ref[...])
    o_ref[...] = _dot(s_hi, w_hi) + _dot(s_lo, w_hi) + _dot(s_hi, w_lo) + b_ref[...]


def _adaln(cond, w, b):
    rows, d = cond.shape
    n = w.shape[1]
    return pl.pallas_call(
        _adaln_body,
        grid=(n // d,),
        in_specs=[pl.BlockSpec((rows, d), lambda j: (0, 0)),
                  pl.BlockSpec((d, d), lambda j: (0, j)),
                  pl.BlockSpec((1, d), lambda j: (0, j))],
        out_specs=pl.BlockSpec((rows, d), lambda j: (0, j)),
        out_shape=jax.ShapeDtypeStruct((rows, n), F32),
        compiler_params=_params(("arbitrary",), 32),
        name="adaln",
    )(cond, w, b)


def _mod_row(i, tiles_per_batch, ctx_row):
    return ctx_row if tiles_per_batch is None else i // tiles_per_batch


def _store_heads(ref, val):
    for h in range(N_HEADS):
        ref[h] = val[:, h * HEAD_DIM:(h + 1) * HEAD_DIM].astype(ref.dtype)


def _proj_body(x_ref, mod_ref, n1_ref, hlb_ref, cw_ref, cb_ref, w_ref, woa_ref,
               za_ref, q_ref, lff_ref, lfb_ref, v_ref, g_ref, sgb_ref,
               *, tiles_per_batch, ctx_row, row_w):
    tm, d = x_ref.shape
    row = _mod_row(pl.program_id(0), tiles_per_batch, ctx_row)
    sh = mod_ref[pl.ds(row, 1), 0:d]
    sc = mod_ref[pl.ds(row, 1), d:2 * d]
    h = (_rms(x_ref[...]) * n1_ref[...] * (1.0 + sc) + sh).astype(BF16)

    def group(g):
        return _dot(h, w_ref[:, g * d:(g + 1) * d])

    def log2_forget(p, first):
        lb = jax.nn.sigmoid(hlb_ref[first:first + 1, :] - hlb_ref[first + 1:first + 2, :])
        return jnp.log2(lb + (1.0 - lb) * jax.nn.sigmoid(p))

    silu = lambda p: p * jax.nn.sigmoid(p)

    u = group(2) * group(0)
    t = lax.broadcasted_iota(jnp.int32, (tm, 1), 0) & (row_w - 1)
    left = jnp.where(t == 0, 0.0, pltpu.roll(u, 1, 0))
    right = jnp.where(t == row_w - 1, 0.0, pltpu.roll(u, tm - 1, 0))
    conv = left * cw_ref[0:1, :] + u * cw_ref[1:2, :] + right * cw_ref[2:3, :] + cb_ref[...]
    ya = _dot((group(1) * conv).astype(BF16), woa_ref[...])
    za_ref[...] = (jax.nn.sigmoid(group(8)) * ya).astype(za_ref.dtype)
    _store_heads(q_ref, silu(group(3)))
    _store_heads(lff_ref, log2_forget(group(4), 0))
    _store_heads(lfb_ref, log2_forget(group(5), 2))
    _store_heads(v_ref, group(6))
    _store_heads(g_ref, silu(group(7)))
    sgb_ref[...] = jax.nn.sigmoid(group(9)).astype(sgb_ref.dtype)


def _proj(x, mod, norm1, hlb, conv_w, conv_b, w_in, w_out_a, *, tokens_per_batch, ctx_row, row_w):
    ntok, d = x.shape
    tm = PROJ_ROWS
    tiles_per_batch = None if tokens_per_batch is None else tokens_per_batch // tm
    tok = pl.BlockSpec((tm, d), lambda i: (i, 0))
    heads = pl.BlockSpec((N_HEADS, tm, HEAD_DIM), lambda i: (0, i, 0))
    full = lambda a: pl.BlockSpec(a.shape, lambda i: (0,) * a.ndim)
    once = lambda a: pl.BlockSpec(a.shape, lambda i: (0,) * a.ndim, pipeline_mode=pl.Buffered(1))
    head_shape = lambda dt: jax.ShapeDtypeStruct((N_HEADS, ntok, HEAD_DIM), dt)
    return pl.pallas_call(
        functools.partial(_proj_body, tiles_per_batch=tiles_per_batch, ctx_row=ctx_row, row_w=row_w),
        grid=(ntok // tm,),
        in_specs=[tok, full(mod), full(norm1), full(hlb), full(conv_w), full(conv_b),
                  once(w_in), once(w_out_a)],
        out_specs=[tok, heads, heads, heads, heads, heads, tok],
        out_shape=[jax.ShapeDtypeStruct((ntok, d), BF16), head_shape(BF16), head_shape(F32),
                   head_shape(F32), head_shape(BF16), head_shape(BF16),
                   jax.ShapeDtypeStruct((ntok, d), BF16)],
        compiler_params=_params(("arbitrary",), 56),
        name="proj",
    )(x, mod, norm1, hlb, conv_w, conv_b, w_in, w_out_a)


def _scan_body(*refs, has_s0, has_sout):
    qf_ref, qb_ref, lff_ref, lfb_ref, vf_ref, vb_ref = refs[:6]
    k = 6
    s0_ref = so_ref = None
    if has_s0:
        s0_ref = refs[k]
        k += 1
    of_ref, ob_ref = refs[k], refs[k + 1]
    k += 2
    if has_sout:
        so_ref = refs[k]
        k += 1
    st_scr, attn_scr = refs[k:k + 2]
    operand_sets = (refs[k + 2:k + 7], refs[k + 7:k + 12])
    step = pl.program_id(1)
    c_len = SCAN_CHUNK
    ts, hd = qf_ref.shape[2], qf_ref.shape[3]
    n_chunks = ts // c_len

    @pl.when(step == 0)
    def _():
        if has_s0:
            for dd in range(2):
                for h in range(N_HEADS):
                    st_scr[dd, h] = s0_ref[0, dd, h].T
        else:
            st_scr[...] = jnp.zeros_like(st_scr)

    tpos = lax.broadcasted_iota(jnp.int32, (c_len, hd), 0)
    row = lax.broadcasted_iota(jnp.int32, (c_len, c_len), 0)
    col = lax.broadcasted_iota(jnp.int32, (c_len, c_len), 1)
    keeps = (row >= col, row <= col)
    dirs = ((0, qf_ref, lff_ref, vf_ref, of_ref), (1, qb_ref, lfb_ref, vb_ref, ob_ref))

    def rows_of(level, dd):
        c = level if dd == 0 else n_chunks - 1 - level
        start = c * c_len
        return c, pl.ds(start if isinstance(start, int) else pl.multiple_of(start, c_len), c_len)

    items = [(dd, h) for dd in range(2) for h in range(N_HEADS)]

    def prepare(level, dd, h, dst):
        qe_scr, qs_scr, ks_scr, ke_scr, eb_scr = dst
        _, q_ref, lf_ref, _, _ = dirs[dd]
        _, sl = rows_of(level, dd)
        lf = lf_ref[h, 0, sl, :]
        b = lf
        s = 1
        while s < c_len:
            if dd == 0:
                b = b + jnp.where(tpos >= s, pltpu.roll(b, s, 0), 0.0)
            else:
                b = b + jnp.where(tpos < c_len - s, pltpu.roll(b, c_len - s, 0), 0.0)
            s *= 2
        far = c_len - 1 if dd == 0 else 0
        half = 0.5 * b[far:far + 1, :]
        e_half = jnp.exp2(half)
        qs = q_ref[h, 0, sl, :].astype(F32) * jnp.exp2(b - half)
        ks = (1.0 - jnp.exp2(lf)) * jnp.exp2(half - b)
        qs_scr[dd, h] = qs.astype(BF16)
        ks_scr[dd, h] = ks.astype(BF16)
        qe_scr[dd, h] = (qs * e_half).astype(BF16)
        ke_scr[dd, h] = (ks * e_half).astype(BF16)
        eb_scr[dd, h] = e_half * e_half

    def scores(dd, h, src):
        _, qs_scr, ks_scr, _, _ = src
        attn = lax.dot_general(qs_scr[dd, h], ks_scr[dd, h], NT_DIMS, preferred_element_type=F32)
        attn_scr[dd, h] = jnp.where(keeps[dd], attn, 0.0).astype(BF16)

    def recur(level, dd, h, src):
        qe_scr, _, _, ke_scr, eb_scr = src
        _, _, _, v_ref, o_ref = dirs[dd]
        _, sl = rows_of(level, dd)
        v = v_ref[h, 0, sl, :]
        st = st_scr[dd, h]
        o = lax.dot_general(qe_scr[dd, h], st.astype(BF16), NT_DIMS, preferred_element_type=F32)
        o_ref[h, 0, sl, :] = (o + _dot(attn_scr[dd, h], v)).astype(o_ref.dtype)
        st_scr[dd, h] = st * eb_scr[dd, h] + lax.dot_general(v, ke_scr[dd, h], TN_DIMS,
                                                             preferred_element_type=F32)

    def level_work(level, src, next_level, dst):
        n_first = len(items) // 2
        for i, (dd, h) in enumerate(items):
            scores(dd, h, src)
            if next_level is not None and i < n_first:
                prepare(next_level, dd, h, dst)
        for i, (dd, h) in enumerate(items):
            recur(level, dd, h, src)
            if next_level is not None and i >= n_first:
                prepare(next_level, dd, h, dst)

    set_a, set_b = operand_sets
    for dd, h in items:
        prepare(0, dd, h, set_a)

    def level_pair(p, carry):
        level_work(2 * p, set_a, 2 * p + 1, set_b)
        level_work(2 * p + 1, set_b, 2 * p + 2, set_a)
        return carry

    lax.fori_loop(0, n_chunks // 2 - 1, level_pair, 0)
    level_work(n_chunks - 2, set_a, n_chunks - 1, set_b)
    level_work(n_chunks - 1, set_b, None, None)

    if has_sout:
        @pl.when(step == pl.num_programs(1) - 1)
        def _():
            for dd in range(2):
                for h in range(N_HEADS):
                    so_ref[0, dd, h] = st_scr[dd, h].T


def _scan(q, lff, lfb, v, s0, *, batch, want_state):
    nh, ntok, hd = q.shape
    t = ntok // batch
    ts = min(SCAN_STEP, t)
    ns = t // ts
    shape4 = (nh, batch, t, hd)
    q, lff, lfb, v = (a.reshape(shape4) for a in (q, lff, lfb, v))
    fwd = pl.BlockSpec((nh, 1, ts, hd), lambda b, s: (0, b, s, 0))
    bwd = pl.BlockSpec((nh, 1, ts, hd), lambda b, s: (0, b, ns - 1 - s, 0))
    state = pl.BlockSpec((1, 2, nh, hd, hd), lambda b, s: (b, 0, 0, 0, 0))
    in_specs = [fwd, bwd, fwd, bwd, fwd, bwd]
    args = [q, q, lff, lfb, v, v]
    if s0 is not None:
        in_specs.append(state)
        args.append(s0)
    out_specs = [fwd, bwd]
    out_shape = [jax.ShapeDtypeStruct(shape4, BF16), jax.ShapeDtypeStruct(shape4, BF16)]
    if want_state:
        out_specs.append(state)
        out_shape.append(jax.ShapeDtypeStruct((batch, 2, nh, hd, hd), F32))
    outs = pl.pallas_call(
        functools.partial(_scan_body, has_s0=s0 is not None, has_sout=want_state),
        grid=(batch, ns),
        in_specs=in_specs,
        out_specs=out_specs,
        out_shape=out_shape,
        scratch_shapes=[pltpu.VMEM((2, nh, hd, hd), F32),
                        pltpu.VMEM((2, nh, SCAN_CHUNK, SCAN_CHUNK), BF16)]
        + ([pltpu.VMEM((2, nh, SCAN_CHUNK, hd), BF16)] * 4 + [pltpu.VMEM((2, nh, 1, hd), F32)]) * 2,
        compiler_params=_params(("arbitrary", "arbitrary"), 32),
        name="scan",
    )(*args)
    o_f = outs[0].reshape(nh, ntok, hd)
    o_b = outs[1].reshape(nh, ntok, hd)
    return o_f, o_b, (outs[2] if want_state else None)


def _post_body(of_ref, ob_ref, g_ref, sgb_ref, za_ref, x_ref, mod_ref, on_ref, n2_ref,
               wob_ref, wo_ref, wrt_ref, x1_ref, h2p_ref, lgt_ref, ob_scr,
               *, tiles_per_batch, ctx_row):
    i = pl.program_id(0)
    tm, d = x_ref.shape
    row = _mod_row(i, tiles_per_batch, ctx_row)
    for h in range(N_HEADS):
        lanes = slice(h * HEAD_DIM, (h + 1) * HEAD_DIM)
        o = _rms(of_ref[h].astype(F32) + ob_ref[h].astype(F32))
        ob_scr[:, lanes] = (o * on_ref[:, lanes] * g_ref[h].astype(F32)).astype(BF16)
    yb = _dot(ob_scr[...], wob_ref[...])
    z = za_ref[...].astype(F32) + sgb_ref[...].astype(F32) * yb
    mix = _dot(z.astype(BF16), wo_ref[...])
    x1 = x_ref[...] + mod_ref[pl.ds(row, 1), 2 * d:3 * d] * mix
    x1_ref[...] = x1
    sh2 = mod_ref[pl.ds(row, 1), 3 * d:4 * d]
    sc2 = mod_ref[pl.ds(row, 1), 4 * d:5 * d]
    h2 = _rms(x1) * n2_ref[...] * (1.0 + sc2) + sh2
    half = d // 2
    top = lax.bitcast_convert_type(h2[:, :half].astype(BF16).astype(F32), jnp.uint32)
    bot = lax.bitcast_convert_type(h2[:, half:].astype(BF16).astype(F32), jnp.uint32)
    h2p_ref[...] = top | (bot >> 16)
    h_hi, h_lo = _split2(h2)
    w_hi, w_lo = _split2(wrt_ref[...])
    nt = lambda a, b: lax.dot_general(a, b, NT_DIMS, preferred_element_type=F32)
    lgt_ref[...] = nt(w_hi, h_hi) + nt(w_lo, h_hi) + nt(w_hi, h_lo)


def _post(o_f, o_b, g, sgb, za, x, mod, onorm, norm2, w_out_b, w_o, w_router_t,
          *, tokens_per_batch, ctx_row):
    ntok, d = x.shape
    tm = PROJ_TILE
    ne = w_router_t.shape[0]
    tiles_per_batch = None if tokens_per_batch is None else tokens_per_batch // tm
    tok = pl.BlockSpec((tm, d), lambda i: (i, 0))
    heads = pl.BlockSpec((N_HEADS, tm, HEAD_DIM), lambda i: (0, i, 0))
    full = lambda a: pl.BlockSpec(a.shape, lambda i: (0,) * a.ndim)
    return pl.pallas_call(
        functools.partial(_post_body, tiles_per_batch=tiles_per_batch, ctx_row=ctx_row),
        grid=(ntok // tm,),
        in_specs=[heads, heads, heads, tok, tok, tok, full(mod), full(onorm), full(norm2),
                  full(w_out_b), full(w_o), full(w_router_t)],
        out_specs=[tok, pl.BlockSpec((tm, d // 2), lambda i: (i, 0)),
                   pl.BlockSpec((ne, tm), lambda i: (0, i))],
        out_shape=[jax.ShapeDtypeStruct((ntok, d), F32),
                   jax.ShapeDtypeStruct((ntok, d // 2), jnp.uint32),
                   jax.ShapeDtypeStruct((ne, ntok), F32)],
        scratch_shapes=[pltpu.VMEM((tm, d), BF16)],
        compiler_params=_params(("arbitrary",), 48),
        name="post",
    )(o_f, o_b, g, sgb, za, x, mod, onorm, norm2, w_out_b, w_o, w_router_t)


def _lane_prefix(mask):
    rows, t = mask.shape
    seg = min(PREFIX_SEG, t)
    r = lax.broadcasted_iota(jnp.int32, (seg, seg), 0)
    c = lax.broadcasted_iota(jnp.int32, (seg, seg), 1)
    tri = (r <= c).astype(BF16)
    carry = jnp.zeros((rows, 1), F32)
    outs = []
    for s in range(t // seg):
        loc = _dot(mask[:, s * seg:(s + 1) * seg].astype(BF16), tri) + carry
        outs.append(loc)
        carry = loc[:, seg - 1:seg]
    return outs[0] if len(outs) == 1 else jnp.concatenate(outs, axis=1)


def _route_body(lg_ref, idx_ref, w_ref, pos_scr, aff_scr, *, cap):
    ne, t = lg_ref.shape
    lg = lg_ref[...]
    e = jnp.exp(lg - jnp.max(lg, axis=0, keepdims=True))
    aff = e / jnp.sum(e, axis=0, keepdims=True)
    bits = lax.bitcast_convert_type(aff, jnp.int32)

    def search(it, tau):
        cand = tau | (jnp.int32(1) << (30 - it))
        cnt = jnp.sum((bits >= cand).astype(F32), axis=1, keepdims=True)
        return jnp.where(cnt >= cap, cand, tau)

    tau = lax.fori_loop(0, 31, search, jnp.zeros((ne, 1), jnp.int32))
    above = bits > tau
    ties = bits == tau
    need = cap - jnp.sum(above.astype(F32), axis=1, keepdims=True)
    sel = above | (ties & (_lane_prefix(ties) <= need))
    pos = jnp.where(sel, _lane_prefix(sel) - 1.0, -1.0)
    for c in range(t // LANES):
        pos_scr[c] = pos[:, c * LANES:(c + 1) * LANES]
        aff_scr[c] = aff[:, c * LANES:(c + 1) * LANES]

    lane = lax.broadcasted_iota(jnp.int32, (1, LANES), 1).astype(F32)
    slot0 = lax.broadcasted_iota(jnp.int32, (LANES, LANES), 0).astype(F32)
    width = min(cap, LANES)

    def expert(ex, carry):
        for sb in range(max(cap // LANES, 1)):
            slot = slot0 + float(sb * LANES)

            def lanes_chunk(c, acc):
                hit = pos_scr[c, pl.ds(ex, 1), :] == slot
                tok = lane + (c * LANES).astype(F32)
                return (acc[0] + jnp.where(hit, tok, 0.0),
                        acc[1] + jnp.where(hit, aff_scr[c, pl.ds(ex, 1), :], 0.0))

            zero = jnp.zeros((LANES, LANES), F32)
            acc_i, acc_w = lax.fori_loop(0, t // LANES, lanes_chunk, (zero, zero))
            row_i = jnp.sum(acc_i.T, axis=0, keepdims=True)
            row_w = jnp.sum(acc_w.T, axis=0, keepdims=True)
            out = slice(sb * LANES, sb * LANES + width)
            idx_ref[0, ex, :, out] = row_i[:, :width].astype(jnp.int32)
            w_ref[0, ex, :, out] = row_w[:, :width]
        return carry

    lax.fori_loop(0, ne, expert, 0)


def _route(logits_t, *, batch):
    ne, ntok = logits_t.shape
    t = ntok // batch
    cap = CAPACITY_FACTOR * t // ne
    out = pl.BlockSpec((1, ne, 1, cap), lambda b: (b, 0, 0, 0))
    return pl.pallas_call(
        functools.partial(_route_body, cap=cap),
        grid=(batch,),
        in_specs=[pl.BlockSpec((ne, t), lambda b: (0, b))],
        out_specs=[out, out],
        out_shape=[jax.ShapeDtypeStruct((batch, ne, 1, cap), jnp.int32),
                   jax.ShapeDtypeStruct((batch, ne, 1, cap), F32)],
        scratch_shapes=[pltpu.VMEM((t // LANES, ne, LANES), F32),
                        pltpu.VMEM((t // LANES, ne, LANES), F32)],
        compiler_params=_params(("arbitrary",), 32),
        name="route",
    )(logits_t)


def _gather_body(idx_sm, h2p_ref, *rest, experts_per_step, cap, ne):
    xs_ref = rest[-1]
    half = xs_ref.shape[2]
    b = pl.program_id(0)
    g = pl.program_id(1)
    sub = lax.broadcasted_iota(jnp.int32, (SUBLANES, half), 0)
    for k in range(experts_per_step):
        base = (b * ne + g * experts_per_step + k) * cap

        def rows(jj, carry, base=base, k=k):
            j0 = pl.multiple_of(jj * SUBLANES, SUBLANES)
            tile = None
            for u in range(SUBLANES):
                src = jnp.broadcast_to(h2p_ref[0, pl.ds(idx_sm[base + j0 + u], 1), :],
                                       (SUBLANES, half))
                tile = src if tile is None else jnp.where(sub == u, src, tile)
            xs_ref[k, pl.ds(j0, SUBLANES), :] = tile
            return carry

        lax.fori_loop(0, cap // SUBLANES, rows, 0)


def _gather(idx, h2p, xs_prev, *, batch, experts_per_step, total_rows, row_offset):
    _, ne, _, cap = idx.shape
    ntok, half = h2p.shape
    t = ntok // batch
    eps = experts_per_step
    off = row_offset // cap
    in_specs = [pl.BlockSpec((1, t, half), lambda b, g, sm: (b, 0, 0))]
    args = [idx.reshape(-1), h2p.reshape(batch, t, half)]
    aliases = {}
    if xs_prev is not None:
        in_specs.append(pl.BlockSpec(memory_space=pl.ANY))
        args.append(xs_prev)
        aliases = {2: 0}
    return pl.pallas_call(
        functools.partial(_gather_body, experts_per_step=eps, cap=cap, ne=ne),
        grid_spec=pltpu.PrefetchScalarGridSpec(
            num_scalar_prefetch=1,
            grid=(batch, ne // eps),
            in_specs=in_specs,
            out_specs=pl.BlockSpec((eps, cap, half), lambda b, g, sm: (g, off + b, 0)),
        ),
        out_shape=jax.ShapeDtypeStruct((ne, total_rows, half), jnp.uint32),
        input_output_aliases=aliases,
        compiler_params=_params(("arbitrary", "arbitrary"), 40),
        name="gather",
    )(*args)


def _ffn_body(xs_ref, wg_ref, wu_ref, wd_ref, ys_ref, wg_scr, wu_scr, wd_scr):
    @pl.when(pl.program_id(1) == 0)
    def _():
        wg_scr[...] = wg_ref[0, 0].astype(BF16)
        wu_scr[...] = wu_ref[0, 0].astype(BF16)
        wd_scr[...] = wd_ref[0, 0].astype(BF16)

    w = xs_ref[0]
    top = lax.bitcast_convert_type(w & jnp.uint32(0xFFFF0000), F32).astype(BF16)
    bot = lax.bitcast_convert_type(w << 16, F32).astype(BF16)
    xs = jnp.concatenate([top, bot], axis=1)
    gate = _dot(xs, wg_scr[...])
    up = _dot(xs, wu_scr[...])
    hid = (gate * jax.nn.sigmoid(gate) * up).astype(BF16)
    ys_ref[0] = _dot(hid, wd_scr[...]).astype(ys_ref.dtype)


def _ffn(xs, w_gate, w_up, w_down):
    ne, rows, half = xs.shape
    d, f = w_gate.shape[2], w_gate.shape[3]
    tr = FFN_ROWS
    wspec = lambda a: pl.BlockSpec((1, 1) + a.shape[2:], lambda e, r: (0, e, 0, 0))
    return pl.pallas_call(
        _ffn_body,
        grid=(ne, rows // tr),
        in_specs=[pl.BlockSpec((1, tr, half), lambda e, r: (e, r, 0)),
                  wspec(w_gate), wspec(w_up), wspec(w_down)],
        out_specs=pl.BlockSpec((1, tr, d), lambda e, r: (e, r, 0)),
        out_shape=jax.ShapeDtypeStruct((ne, rows, d), BF16),
        scratch_shapes=[pltpu.VMEM((d, f), BF16), pltpu.VMEM((d, f), BF16), pltpu.VMEM((f, d), BF16)],
        compiler_params=_params(("arbitrary", "arbitrary"), 56),
        name="ffn",
    )(xs, w_gate, w_up, w_down)


def _scatter_body(idx_sm, w_sm, ys_ref, y_ref, row_scr, *, experts_per_step, cap, ne):
    b = pl.program_id(0)
    g = pl.program_id(1)
    d = y_ref.shape[2]
    group = min(SUBLANES, cap // SUBLANES)
    stride = cap // group
    sub = lax.broadcasted_iota(jnp.int32, (SUBLANES, d), 0)

    @pl.when(g == 0)
    def _():
        y_ref[...] = jnp.zeros_like(y_ref)

    for k in range(experts_per_step):
        row_scr[...] = ys_ref[k].astype(F32)
        base = (b * ne + g * experts_per_step + k) * cap

        def rows(j, carry, base=base):
            pending = []
            for u in range(group):
                slot = j + u * stride
                tok = idx_sm[base + slot]
                t0 = pl.multiple_of((tok // SUBLANES) * SUBLANES, SUBLANES)
                src = jnp.broadcast_to(row_scr[pl.ds(slot, 1), :], (SUBLANES, d)) * w_sm[base + slot]
                tile = y_ref[0, pl.ds(t0, SUBLANES), :]
                pending.append((t0, tile + jnp.where(sub == tok % SUBLANES, src, 0.0)))
            for t0, val in pending:
                y_ref[0, pl.ds(t0, SUBLANES), :] = val
            return carry

        lax.fori_loop(0, stride, rows, 0)


def _scatter(idx, w, ys, *, batch, tokens_per_batch, experts_per_step, row_offset):
    _, ne, _, cap = idx.shape
    d = ys.shape[2]
    t = tokens_per_batch
    eps = experts_per_step
    off = row_offset // cap
    return pl.pallas_call(
        functools.partial(_scatter_body, experts_per_step=eps, cap=cap, ne=ne),
        grid_spec=pltpu.PrefetchScalarGridSpec(
            num_scalar_prefetch=2,
            grid=(batch, ne // eps),
            in_specs=[pl.BlockSpec((eps, cap, d), lambda b, g, i_sm, w_sm: (g, off + b, 0))],
            out_specs=pl.BlockSpec((1, t, d), lambda b, g, i_sm, w_sm: (b, 0, 0)),
            scratch_shapes=[pltpu.VMEM((cap, d), F32)],
        ),
        out_shape=jax.ShapeDtypeStruct((batch, t, d), F32),
        compiler_params=_params(("arbitrary", "arbitrary"), 48),
        name="scatter",
    )(idx.reshape(-1), w.reshape(-1), ys)


def _final_body(x1_ref, y_ref, mod_ref, fn_ref, o_ref, *, tiles_per_batch, ctx_row):
    d = x1_ref.shape[1]
    row = _mod_row(pl.program_id(0), tiles_per_batch, ctx_row)
    x = x1_ref[...] + mod_ref[pl.ds(row, 1), 5 * d:6 * d] * y_ref[...]
    o_ref[...] = _rms(x) * fn_ref[...]


def _final(x1, y, mod, final_norm, *, tokens_per_batch, ctx_row):
    ntok, d = x1.shape
    tm = PROJ_TILE
    tiles_per_batch = None if tokens_per_batch is None else tokens_per_batch // tm
    tok = pl.BlockSpec((tm, d), lambda i: (i, 0))
    full = lambda a: pl.BlockSpec(a.shape, lambda i: (0,) * a.ndim)
    return pl.pallas_call(
        functools.partial(_final_body, tiles_per_batch=tiles_per_batch, ctx_row=ctx_row),
        grid=(ntok // tm,),
        in_specs=[tok, tok, full(mod), full(final_norm)],
        out_specs=tok,
        out_shape=jax.ShapeDtypeStruct((ntok, d), F32),
        compiler_params=_params(("arbitrary",), 32),
        name="final",
    )(x1, y, mod, final_norm)


def kernel(x_prompt, x_sample, state_hgrn, c, c_ctx, w_ada, b_ada, norm1, w_in, conv_w, conv_b,
           hgrn_lb, hgrn_onorm, w_out_a, w_out_b, w_o, norm2, w_router, w_e_gate, w_e_up, w_e_down,
           final_norm):
    assert w_ada.shape[0] == 1, "single-layer trunk"
    bp, tp, d = x_prompt.shape
    bs, ts, _ = x_sample.shape
    grid_w = 64
    ne = w_router.shape[2]
    cap_p = CAPACITY_FACTOR * tp // ne
    cap_s = CAPACITY_FACTOR * ts // ne

    ctx_row = bs
    pad = (-(bs + 1)) % 8
    cond = jnp.concatenate([c, c_ctx[None, :], jnp.zeros((pad, d), F32)], axis=0)
    mod = _adaln(cond, w_ada[0], b_ada)

    w_in_p, woa, wob, wo = (a[0].astype(BF16) for a in (w_in, w_out_a, w_out_b, w_o))
    hlb = hgrn_lb.reshape(-1, d)
    w_router_t = w_router[0].T

    def mix_half(x, s0, tokens_per_batch, row_w, want_state):
        batch = x.shape[0]
        xf = x.reshape(-1, d)
        za, q, lff, lfb, v, g, sgb = _proj(xf, mod, norm1, hlb, conv_w[0], conv_b, w_in_p, woa,
                                           tokens_per_batch=tokens_per_batch, ctx_row=ctx_row,
                                           row_w=row_w)
        o_f, o_b, s_new = _scan(q, lff, lfb, v, s0, batch=batch, want_state=want_state)
        x1, h2p, lgt = _post(o_f, o_b, g, sgb, za, xf, mod, hgrn_onorm, norm2, wob, wo, w_router_t,
                             tokens_per_batch=tokens_per_batch, ctx_row=ctx_row)
        idx, w = _route(lgt, batch=batch)
        return x1, h2p, idx, w, s_new

    x1p, h2pp, idxp, wp, s_ctx = mix_half(x_prompt, None, None, tp, True)
    x1s, h2ps, idxs, ws, _ = mix_half(x_sample, state_hgrn[:, 0], ts, grid_w, False)

    rows_p = bp * cap_p
    total_rows = rows_p + bs * cap_s
    xs = _gather(idxp, h2pp, None, batch=bp, experts_per_step=ne, total_rows=total_rows, row_offset=0)
    xs = _gather(idxs, h2ps, xs, batch=bs, experts_per_step=1, total_rows=total_rows,
                 row_offset=rows_p)
    ys = _ffn(xs, w_e_gate, w_e_up, w_e_down)
    yp = _scatter(idxp, wp, ys, batch=bp, tokens_per_batch=tp, experts_per_step=ne, row_offset=0)
    ysm = _scatter(idxs, ws, ys, batch=bs, tokens_per_batch=ts, experts_per_step=1,
                   row_offset=rows_p)

    y_prompt = _final(x1p, yp.reshape(-1, d), mod, final_norm[None, :], tokens_per_batch=None,
                      ctx_row=ctx_row)
    y_sample = _final(x1s, ysm.reshape(-1, d), mod, final_norm[None, :], tokens_per_batch=ts,
                      ctx_row=ctx_row)
    return (y_prompt.reshape(bp, tp, d), y_sample.reshape(bs, ts, d), s_ctx[:, None])
```
